```python
import math
import jax, jax.numpy as jnp
from jax import lax
import numpy as np

D_MODEL = 1024
BATCH = 32
SEQ = 2048
DEPTH = 4

CTX_LEN = 256
GRID_W = 64
HEAD_DIM = 64
N_HEADS_A = (D_MODEL // 2) // HEAD_DIM
N_KV_A = N_HEADS_A // 4
GQA_GROUP = N_HEADS_A // N_KV_A
WINDOW = 128
BLOCK = 128
BAND = BLOCK + 2 * WINDOW
CONV_CH = D_MODEL // 4
CONV_W = 3
DIFF_DIM = 32
N_HEADS_C = (D_MODEL // 4) // (2 * DIFF_DIM)
N_EXPERTS = 16
EXPERT_FF = D_MODEL
CAPACITY_FACTOR = 2
ROPE_THETA = 10000.0
EPS = 1e-6
NEG_INF = -1e30

A_Q = N_HEADS_A * HEAD_DIM
A_KV = N_KV_A * HEAD_DIM
C_QK = N_HEADS_C * 2 * DIFF_DIM
C_V = N_HEADS_C * 2 * DIFF_DIM
PROJ_SPLITS = (A_Q, A_KV, A_KV, CONV_CH, CONV_CH, CONV_CH, C_QK, C_QK, C_V)
PROJ_DIM = A_Q + 2 * A_KV + 3 * CONV_CH + 2 * C_QK + C_V
MIX_DIM = A_Q + CONV_CH + C_V

kernel_name = 'hybrid_parallel_heads_ec_moe_dit'


def _rmsnorm(x, g):
    xf = x.astype(jnp.float32)
    y = xf * lax.rsqrt(jnp.mean(xf * xf, axis=-1, keepdims=True) + EPS)
    return (y * g.astype(jnp.float32)).astype(x.dtype)


def _split_proj(p):
    outs, start = [], 0
    for w in PROJ_SPLITS:
        outs.append(p[..., start:start + w])
        start += w
    return outs


def _axial_angles(L, n):
    rows = L // GRID_W
    row = jnp.broadcast_to(jnp.arange(rows, dtype=jnp.float32)[:, None], (rows, GRID_W)).reshape(-1)
    col = jnp.broadcast_to(jnp.arange(GRID_W, dtype=jnp.float32)[None, :], (rows, GRID_W)).reshape(-1)
    inv = ROPE_THETA ** (-jnp.arange(0, n, 2, dtype=jnp.float32) / n)
    return row[:, None] * inv[None, :], col[:, None] * inv[None, :]


def _rotate(x, ang):
    shape = (1, ang.shape[0]) + (1,) * (x.ndim - 3) + (ang.shape[1],)
    cos = jnp.cos(ang).reshape(shape).astype(x.dtype)
    sin = jnp.sin(ang).reshape(shape).astype(x.dtype)
    x1, x2 = jnp.split(x, 2, axis=-1)
    return jnp.concatenate([x1 * cos - x2 * sin, x2 * cos + x1 * sin], axis=-1)


def _rope_2d(x):
    L, d = x.shape[1], x.shape[-1]
    ang_r, ang_c = _axial_angles(L, d // 2)
    xr, xc = jnp.split(x, 2, axis=-1)
    return jnp.concatenate([_rotate(xr, ang_r), _rotate(xc, ang_c)], axis=-1)


def _window_gqa(q, k, v, kc, vc, sink):
    B, L, H, dh = q.shape
    Lc = kc.shape[1]
    nblk = L // BLOCK
    qb = jnp.moveaxis((q * dh ** -0.5).reshape(B, nblk, BLOCK, N_KV_A, GQA_GROUP, dh), 1, 0)
    pad = ((0, 0), (WINDOW, WINDOW), (0, 0), (0, 0))
    kp = jnp.pad(k, pad)
    vp = jnp.pad(v, pad)
    sink_l = jnp.broadcast_to(sink.astype(jnp.float32).reshape(1, N_KV_A, GQA_GROUP, 1, 1),
                              (B, N_KV_A, GQA_GROUP, BLOCK, 1))

    def block(args):
        qi, i = args
        start = i * BLOCK
        kb = lax.dynamic_slice_in_dim(kp, start, BAND, axis=1)
        vb = lax.dynamic_slice_in_dim(vp, start, BAND, axis=1)
        qpos = start + jnp.arange(BLOCK)
        kpos = start - WINDOW + jnp.arange(BAND)
        ok = (jnp.abs(qpos[:, None] - kpos[None, :]) <= WINDOW) & (kpos >= 0)[None, :] & (kpos < L)[None, :]
        s_loc = jnp.einsum('bqkgd,bskd->bkgqs', qi, kb).astype(jnp.float32)
        s_loc = jnp.where(ok, s_loc, NEG_INF)
        s_ctx = jnp.einsum('bqkgd,bckd->bkgqc', qi, kc).astype(jnp.float32)
        p = jax.nn.softmax(jnp.concatenate([s_ctx, s_loc, sink_l], axis=-1), axis=-1).astype(v.dtype)
        return (jnp.einsum('bkgqc,bckd->bqkgd', p[..., :Lc], vc)
                + jnp.einsum('bkgqs,bskd->bqkgd', p[..., Lc:Lc + BAND], vb))

    out = lax.map(block, (qb, jnp.arange(nblk)))
    return jnp.moveaxis(out, 0, 1).reshape(B, L, H * dh)


def _ctx_gqa(qc, kc, vc, sink):
    B, Lc, H, dh = qc.shape
    qg = (qc * dh ** -0.5).reshape(B, Lc, N_KV_A, GQA_GROUP, dh)
    s = jnp.einsum('bqkgd,bckd->bkgqc', qg, kc).astype(jnp.float32)
    sink_l = jnp.broadcast_to(sink.astype(jnp.float32).reshape(1, N_KV_A, GQA_GROUP, 1, 1),
                              (B, N_KV_A, GQA_GROUP, Lc, 1))
    p = jax.nn.softmax(jnp.concatenate([s, sink_l], axis=-1), axis=-1)[..., :Lc].astype(vc.dtype)
    return jnp.einsum('bkgqc,bckd->bqkgd', p, vc).reshape(B, Lc, H * dh)


def _short_conv(h, gb, gc, w):
    L = h.shape[1]
    half = CONV_W // 2
    up = jnp.pad(gc * h, ((0, 0), (half, half), (0, 0)))
    y = sum(up[:, j:j + L] * w[j] for j in range(CONV_W))
    return gb * y


def _diff_weights(qi, K, lam):
    d = qi.shape[-1]
    s = jnp.einsum('bqhjd,bkhjd->bhjqk', qi * d ** -0.5, K).astype(jnp.float32)
    p = jax.nn.softmax(s, axis=-1)
    return p[:, :, 0] - lam * p[:, :, 1]


def _diff_attn_latent(q, k, v, kc, vc, lam):
    B, L, H, _, d = q.shape
    K = jnp.concatenate([kc, k], axis=1)
    V = jnp.concatenate([vc, v], axis=1)
    qb = jnp.moveaxis(q.reshape(B, L // BLOCK, BLOCK, H, 2, d), 1, 0)

    def block(qi):
        a = _diff_weights(qi, K, lam).astype(V.dtype)
        return jnp.einsum('bhqk,bkhe->bqhe', a, V)

    out = lax.map(block, qb)
    return jnp.moveaxis(out, 0, 1).reshape(B, L, H, 2 * d)


def _diff_attn_ctx(qc, kc, vc, lam):
    a = _diff_weights(qc, kc, lam).astype(vc.dtype)
    return jnp.einsum('bhqk,bkhe->bqhe', a, vc)


def _mixer(hx, hc, w_in, w_out, sink, conv_w, lam_p, subln, lam_init, need_ctx):
    B, L, _ = hx.shape
    Lc = hc.shape[1]
    qa, ka, va, cin, cb, cc, qd, kd, vd = _split_proj(hx @ w_in)
    qa_c, ka_c, va_c, cin_c, cb_c, cc_c, qd_c, kd_c, vd_c = _split_proj(hc @ w_in)
    ka_c = ka_c.reshape(B, Lc, N_KV_A, HEAD_DIM)
    va_c = va_c.reshape(B, Lc, N_KV_A, HEAD_DIM)
    ya = _window_gqa(_rope_2d(qa.reshape(B, L, N_HEADS_A, HEAD_DIM)),
                     _rope_2d(ka.reshape(B, L, N_KV_A, HEAD_DIM)),
                     va.reshape(B, L, N_KV_A, HEAD_DIM), ka_c, va_c, sink)
    yb = _short_conv(cin, cb, cc, conv_w)
    lp = lam_p.astype(jnp.float32)
    lam = jnp.exp(jnp.sum(lp[0] * lp[1])) - jnp.exp(jnp.sum(lp[2] * lp[3])) + lam_init
    kd_c = kd_c.reshape(B, Lc, N_HEADS_C, 2, DIFF_DIM)
    vd_c = vd_c.reshape(B, Lc, N_HEADS_C, 2 * DIFF_DIM)
    yd = _diff_attn_latent(_rope_2d(qd.reshape(B, L, N_HEADS_C, 2, DIFF_DIM)),
                           _rope_2d(kd.reshape(B, L, N_HEADS_C, 2, DIFF_DIM)),
                           vd.reshape(B, L, N_HEADS_C, 2 * DIFF_DIM), kd_c, vd_c, lam)
    yd = (_rmsnorm(yd, subln) * (1.0 - lam_init)).reshape(B, L, C_V)
    out_x = jnp.concatenate([ya, yb, yd], axis=-1) @ w_out
    if not need_ctx:
        return out_x, None
    yac = _ctx_gqa(qa_c.reshape(B, Lc, N_HEADS_A, HEAD_DIM), ka_c, va_c, sink)
    ybc = _short_conv(cin_c, cb_c, cc_c, conv_w)
    ydc = _diff_attn_ctx(qd_c.reshape(B, Lc, N_HEADS_C, 2, DIFF_DIM), kd_c, vd_c, lam)
    ydc = (_rmsnorm(ydc, subln) * (1.0 - lam_init)).reshape(B, Lc, C_V)
    out_c = jnp.concatenate([yac, ybc, ydc], axis=-1) @ w_out
    return out_x, out_c


def _ec_moe(h, router, w_gate, w_up, w_down):
    B, n, _ = h.shape
    cap = CAPACITY_FACTOR * n // N_EXPERTS
    aff = jax.nn.softmax((h @ router).astype(jnp.float32), axis=-1)
    g, idx = lax.top_k(jnp.swapaxes(aff, 1, 2), cap)
    bidx = jnp.arange(B)[:, None, None]
    xs = h[bidx, idx]
    a = jnp.einsum('becd,edf->becf', xs, w_gate)
    u = jnp.einsum('becd,edf->becf', xs, w_up)
    y = jnp.einsum('becf,efd->becd', jax.nn.silu(a) * u, w_down) * g[..., None].astype(h.dtype)
    return jnp.zeros_like(h).at[bidx, idx].add(y)


def setup_inputs(seed: int = 0) -> dict:
    key = jax.random.key(seed)
    ks = jax.random.split(key, 20)
    D = D_MODEL
    f32 = jnp.float32

    def nrm(k, shape, s):
        return jax.random.normal(k, shape, f32) * s

    return {
        'x': nrm(ks[0], (BATCH, SEQ, D), 1.0),
        'c': nrm(ks[1], (BATCH, D), 1.0),
        'ctx': nrm(ks[2], (BATCH, CTX_LEN, D), 1.0),
        'c_ctx': nrm(ks[3], (D,), 1.0),
        'w_ada': nrm(ks[4], (DEPTH, D, 6 * D), 0.5 * D ** -0.5),
        'b_ada': nrm(ks[5], (DEPTH, 6 * D), 0.02),
        'norm1': 1.0 + nrm(ks[6], (DEPTH, D), 0.02),
        'norm2': 1.0 + nrm(ks[7], (DEPTH, D), 0.02),
        'w_in': nrm(ks[8], (DEPTH, D, PROJ_DIM), D ** -0.5),
        'w_out': nrm(ks[9], (DEPTH, MIX_DIM, D), MIX_DIM ** -0.5),
        'attn_sink': nrm(ks[10], (DEPTH, N_HEADS_A), 0.5),
        'conv_w': nrm(ks[11], (DEPTH, CONV_W, CONV_CH), CONV_W ** -0.5),
        'diff_lambda': nrm(ks[12], (DEPTH, 4, DIFF_DIM), 0.1),
        'diff_subln': 1.0 + nrm(ks[13], (DEPTH, 2 * DIFF_DIM), 0.02),
        'router': nrm(ks[14], (DEPTH, D, N_EXPERTS), D ** -0.5),
        'w_gate': nrm(ks[15], (DEPTH, N_EXPERTS, D, EXPERT_FF), D ** -0.5),
        'w_up': nrm(ks[16], (DEPTH, N_EXPERTS, D, EXPERT_FF), D ** -0.5),
        'w_down': nrm(ks[17], (DEPTH, N_EXPERTS, EXPERT_FF, D), EXPERT_FF ** -0.5),
        'norm_final': 1.0 + nrm(ks[18], (D,), 0.02),
    }


def reference(x, c, ctx, c_ctx, w_ada, b_ada, norm1, norm2, w_in, w_out, attn_sink, conv_w,
              diff_lambda, diff_subln, router, w_gate, w_up, w_down, norm_final):
    silu_c = jax.nn.silu(c)
    silu_cc = jax.nn.silu(c_ctx)
    for l in range(DEPTH):
        need_ctx = l < DEPTH - 1
        mod_x = silu_c @ w_ada[l] + b_ada[l]
        mod_c = silu_cc @ w_ada[l] + b_ada[l]
        sh1, sc1, g1, sh2, sc2, g2 = jnp.split(mod_x[:, None, :], 6, axis=-1)
        csh1, csc1, cg1, csh2, csc2, cg2 = jnp.split(mod_c, 6, axis=-1)
        lam_init = 0.8 - 0.6 * math.exp(-0.3 * l)
        hx = _rmsnorm(x, norm1[l]) * (1.0 + sc1) + sh1
        hc = _rmsnorm(ctx, norm1[l]) * (1.0 + csc1) + csh1
        ox, oc = _mixer(hx, hc, w_in[l], w_out[l], attn_sink[l], conv_w[l], diff_lambda[l],
                        diff_subln[l], lam_init, need_ctx)
        x = x + g1 * ox
        hx2 = _rmsnorm(x, norm2[l]) * (1.0 + sc2) + sh2
        x = x + g2 * _ec_moe(hx2, router[l], w_gate[l], w_up[l], w_down[l])
        if need_ctx:
            ctx = ctx + cg1 * oc
            hc2 = _rmsnorm(ctx, norm2[l]) * (1.0 + csc2) + csh2
            ctx = ctx + cg2 * _ec_moe(hc2, router[l], w_gate[l], w_up[l], w_down[l])
    return _rmsnorm(x, norm_final)
```

```python
import functools
import math

import jax
import jax.numpy as jnp
import numpy as np
from jax import lax
from jax.experimental import pallas as pl
from jax.experimental.pallas import tpu as pltpu

D_MODEL = 1024
DEPTH = 4
GRID_W = 64
HEAD_DIM = 64
N_HEADS_A = 8
N_KV_A = 2
WINDOW = 128
BLOCK = 128
CONV_CH = 256
DIFF_DIM = 32
N_HEADS_C = 4
N_EXPERTS = 16
CAPACITY_FACTOR = 2
ROPE_THETA = 10000.0
EPS = 1e-6
NEG_INF = -1e30

PROJ_DIM = 2304
COL_QA, COL_KV_A, COL_QD, COL_KD, COL_CONV = 0, 512, 768, 1024, 1536
LANES = 128
ROW_TILE = 256
MOD_ROWS = 40
VMEM_LIMIT = 56 * 1024 * 1024

F32 = jnp.float32
BF16 = jnp.bfloat16


def _in_col_perm():
    qa = [h * HEAD_DIM + d for c in range(4) for h in (c, 4 + c) for d in range(HEAD_DIM)]
    rest = (list(range(512, 768)) + list(range(1536, 2304)) + list(range(768, 1536)))
    return np.asarray(qa + rest, dtype=np.int32)


def _out_row_perm():
    qa = [h * HEAD_DIM + d for c in range(4) for h in (c, 4 + c) for d in range(HEAD_DIM)]
    return np.asarray(qa + list(range(512, 1024)), dtype=np.int32)


def _rope_tables(ctx_len, seq):
    t = np.arange(seq)
    pos = {0: (t // GRID_W).astype(np.float64), 1: (t % GRID_W).astype(np.float64)}

    def build(group, n_freq):
        inv = ROPE_THETA ** (-np.arange(0, 2 * n_freq, 2, dtype=np.float64) / (2 * n_freq))
        cos = np.ones((ctx_len + seq, LANES), np.float64)
        sin = np.zeros((ctx_len + seq, LANES), np.float64)
        for lane in range(LANES):
            m = lane % group
            axis = 0 if m < group // 2 else 1
            mm = m % (group // 2)
            first = mm < n_freq
            ang = pos[axis] * inv[mm % n_freq]
            cos[ctx_len:, lane] = np.cos(ang)
            sin[ctx_len:, lane] = -np.sin(ang) if first else np.sin(ang)
        return jnp.asarray(cos, F32), jnp.asarray(sin, F32)

    cos_a, sin_a = build(HEAD_DIM, HEAD_DIM // 4)
    cos_d, sin_d = build(DIFF_DIM, DIFF_DIM // 4)
    return cos_a, sin_a, cos_d, sin_d


def _ada_kernel(c_ref, w_ref, b_ref, o_ref):
    c = c_ref[...]
    s = (c * (1.0 / (1.0 + jnp.exp(-c)))).astype(BF16)
    o_ref[...] = jnp.dot(s, w_ref[...].astype(BF16), preferred_element_type=F32) + b_ref[...]


def _ada_call(cs, w_ada, b_ada):
    depth, d, n = w_ada.shape
    tn = 1024
    return pl.pallas_call(
        _ada_kernel,
        grid=(depth, n // tn),
        in_specs=[
            pl.BlockSpec((MOD_ROWS, d), lambda l, k: (0, 0)),
            pl.BlockSpec((None, d, tn), lambda l, k: (l, 0, k)),
            pl.BlockSpec((None, 1, tn), lambda l, k: (l, 0, k)),
        ],
        out_specs=pl.BlockSpec((None, MOD_ROWS, tn), lambda l, k: (l, 0, k)),
        out_shape=jax.ShapeDtypeStruct((depth, MOD_ROWS, n), F32),
        compiler_params=pltpu.CompilerParams(
            dimension_semantics=("arbitrary", "arbitrary"), vmem_limit_bytes=VMEM_LIMIT),
        name="ada_mod",
    )(cs, w_ada, b_ada.reshape(depth, 1, n))


def _rope(v, cos, sin, first, shift):
    fwd = pltpu.roll(v, LANES - shift, axis=1)
    bwd = pltpu.roll(v, shift, axis=1)
    return v * cos + jnp.where(first, fwd, bwd) * sin


def _inproj_kernel(x_ref, sh_ref, sc_ref, nw_ref, w_ref, cosa_ref, sina_ref, cosd_ref, sind_ref,
                   o_ref, *, qa_scale, qd_scale):
    x = x_ref[...]
    ms = jnp.mean(x * x, axis=-1, keepdims=True)
    h = (x * lax.rsqrt(ms + EPS) * nw_ref[...]) * (1.0 + sc_ref[...]) + sh_ref[...]
    hb = h.astype(BF16)
    lane = lax.broadcasted_iota(jnp.int32, (x.shape[0], LANES), 1)
    first_a = (lane % 32) < 16
    first_d = (lane % 16) < 8
    cosa, sina, cosd, sind = cosa_ref[...], sina_ref[...], cosd_ref[...], sind_ref[...]

    def rope_a(v):
        return _rope(v, cosa, sina, first_a, 16)

    def rope_d(v):
        return _rope(v, cosd, sind, first_d, 8)

    for c in range(PROJ_DIM // 256):
        acc = jnp.dot(hb, w_ref[:, 256 * c:256 * (c + 1)], preferred_element_type=F32)
        lo, hi = acc[:, :LANES], acc[:, LANES:]
        if c in (0, 1):
            lo, hi = rope_a(lo) * qa_scale, rope_a(hi) * qa_scale
        elif c == 2:
            lo = rope_a(lo)
        elif c == 3:
            lo, hi = rope_d(lo) * qd_scale, rope_d(hi) * qd_scale
        elif c == 4:
            lo, hi = rope_d(lo), rope_d(hi)
        o_ref[:, 256 * c:256 * c + LANES] = lo.astype(BF16)
        o_ref[:, 256 * c + LANES:256 * (c + 1)] = hi.astype(BF16)


def _mod_spec(nb, k):
    return pl.BlockSpec((None, 1, D_MODEL), lambda b, j: (jnp.where(j == 0, nb, b), 0, k))


def _inproj_call(x_all, mod_l, nw, w_in_b, tables):
    nb, ltot, d = x_all.shape
    tm = ROW_TILE
    tab_spec = pl.BlockSpec((tm, LANES), lambda b, j: (j, 0))
    kern = functools.partial(_inproj_kernel, qa_scale=HEAD_DIM ** -0.5, qd_scale=DIFF_DIM ** -0.5)
    return pl.pallas_call(
        kern,
        grid=(nb, ltot // tm),
        in_specs=[
            pl.BlockSpec((None, tm, d), lambda b, j: (b, j, 0)),
            _mod_spec(nb, 0), _mod_spec(nb, 1),
            pl.BlockSpec((1, d), lambda b, j: (0, 0)),
            pl.BlockSpec((d, PROJ_DIM), lambda b, j: (0, 0)),
            tab_spec, tab_spec, tab_spec, tab_spec,
        ],
        out_specs=pl.BlockSpec((None, tm, PROJ_DIM), lambda b, j: (b, j, 0)),
        out_shape=jax.ShapeDtypeStruct((nb, ltot, PROJ_DIM), BF16),
        compiler_params=pltpu.CompilerParams(
            dimension_semantics=("parallel", "arbitrary"), vmem_limit_bytes=VMEM_LIMIT),
        name="in_proj",
    )(x_all, mod_l, mod_l, nw, w_in_b, *tables)


def _win_kernel(sink_ref, q_ref, kv_ref, o_ref, *, ctx_len, ltot):
    j = pl.program_id(1)
    n_ctx_blk = ctx_len // BLOCK
    is_lat = j >= n_ctx_blk
    i = jnp.maximum(j - n_ctx_blk, 0)
    band = BLOCK + 2 * WINDOW
    start = pl.multiple_of(jnp.minimum(ctx_len - WINDOW + BLOCK * i, ltot - band), BLOCK)
    k = jnp.concatenate([kv_ref[0:ctx_len, 0:LANES], kv_ref[pl.ds(start, band), 0:LANES]], axis=0)
    v = jnp.concatenate([kv_ref[0:ctx_len, LANES:], kv_ref[pl.ds(start, band), LANES:]], axis=0)
    q = q_ref[...]
    lane = lax.broadcasted_iota(jnp.int32, (BLOCK, LANES), 1)
    lo = lane < HEAD_DIM
    zero = jnp.zeros((BLOCK, LANES), BF16)
    parts = []
    for c in range(4):
        qc = q[:, LANES * c:LANES * (c + 1)]
        parts += [jnp.where(lo, qc, zero), jnp.where(lo, zero, qc)]
    lhs = jnp.concatenate(parts, axis=0)
    s = lax.dot_general(lhs, k, (((1,), (1,)), ((), ())), preferred_element_type=F32)
    nk = ctx_len + band
    row = lax.broadcasted_iota(jnp.int32, (BLOCK, nk), 0)
    col = lax.broadcasted_iota(jnp.int32, (BLOCK, nk), 1)
    kpos = col + (start - 2 * ctx_len)
    dist = jnp.abs(BLOCK * i + row - kpos)
    lat = jnp.where(is_lat, 1, 0)
    ok_band = jnp.where(dist <= WINDOW, jnp.where(kpos >= 0, lat, 0), 0)
    ok = jnp.where(col < ctx_len, 1, ok_band) > 0
    outs = []
    for r in range(8):
        head = r // 2 if r % 2 == 0 else 4 + r // 2
        sk = sink_ref[head]
        sr = jnp.where(ok, s[BLOCK * r:BLOCK * (r + 1)], NEG_INF)
        m = jnp.maximum(jnp.max(sr, axis=-1, keepdims=True), sk)
        e = jnp.exp(sr - m)
        l = jnp.sum(e, axis=-1, keepdims=True) + jnp.exp(sk - m)
        outs.append(jnp.dot(e.astype(BF16), v, preferred_element_type=F32) * (1.0 / l))
    for c in range(4):
        o_ref[:, LANES * c:LANES * (c + 1)] = jnp.where(lo, outs[2 * c], outs[2 * c + 1]).astype(BF16)


def _win_call(p_all, sink, ctx_len):
    nb, ltot, _ = p_all.shape
    kern = functools.partial(_win_kernel, ctx_len=ctx_len, ltot=ltot)
    return pl.pallas_call(
        kern,
        grid=(nb, ltot // BLOCK),
        in_specs=[
            pl.BlockSpec(memory_space=pltpu.SMEM),
            pl.BlockSpec((None, BLOCK, 512), lambda b, j: (b, j, 0)),
            pl.BlockSpec((None, ltot, 256), lambda b, j: (b, 0, COL_KV_A // 256)),
        ],
        out_specs=pl.BlockSpec((None, BLOCK, 512), lambda b, j: (b, j, 0)),
        out_shape=jax.ShapeDtypeStruct((nb, ltot, 512), BF16),
        compiler_params=pltpu.CompilerParams(
            dimension_semantics=("parallel", "arbitrary"), vmem_limit_bytes=VMEM_LIMIT),
        name="win_gqa",
    )(sink, p_all, p_all)


def _conv_kernel(p_ref, w_ref, o_ref, *, ctx_len, ltot):
    w = w_ref[...]
    for s0, s1 in ((0, ctx_len), (ctx_len, ltot)):
        n = s1 - s0
        cin = p_ref[s0:s1, 0:256].astype(F32)
        cb = p_ref[s0:s1, 256:512].astype(F32)
        cc = p_ref[s0:s1, 512:768].astype(F32)
        u = cc * cin
        row = lax.broadcasted_iota(jnp.int32, (n, CONV_CH), 0)
        prev = jnp.where(row == 0, 0.0, pltpu.roll(u, 1, axis=0))
        nxt = jnp.where(row == n - 1, 0.0, pltpu.roll(u, n - 1, axis=0))
        y = prev * w[0:1] + u * w[1:2] + nxt * w[2:3]
        o_ref[s0:s1, :] = (cb * y).astype(BF16)


def _conv_call(p_all, conv_w, ctx_len):
    nb, ltot, _ = p_all.shape
    kern = functools.partial(_conv_kernel, ctx_len=ctx_len, ltot=ltot)
    return pl.pallas_call(
        kern,
        grid=(nb,),
        in_specs=[
            pl.BlockSpec((None, ltot, 768), lambda b: (b, 0, COL_CONV // 768)),
            pl.BlockSpec((3, CONV_CH), lambda b: (0, 0)),
        ],
        out_specs=pl.BlockSpec((None, ltot, CONV_CH), lambda b: (b, 0, 0)),
        out_shape=jax.ShapeDtypeStruct((nb, ltot, CONV_CH), BF16),
        compiler_params=pltpu.CompilerParams(
            dimension_semantics=("parallel",), vmem_limit_bytes=VMEM_LIMIT),
        name="short_conv",
    )(p_all, conv_w)


def _diff_kernel(lam_ref, q_ref, kv_ref, g_ref, o_ref, *, ctx_len, ltot):
    j = pl.program_id(1)
    tq = q_ref.shape[0]

    def body(nk):
        q = q_ref[...]
        k = kv_ref[0:nk, 0:256]
        v = kv_ref[0:nk, 256:512]
        lam = lam_ref[0]
        lane = lax.broadcasted_iota(jnp.int32, (tq, 256), 1)
        grp32 = lane >> 5
        grp64 = lane >> 6
        zero = jnp.zeros((tq, 256), BF16)
        acc = jnp.zeros((tq, 256), F32)
        for h in range(N_HEADS_C):
            lhs = jnp.concatenate([jnp.where(grp32 == 2 * h, q, zero),
                                   jnp.where(grp32 == 2 * h + 1, q, zero)], axis=0)
            s = lax.dot_general(lhs, k, (((1,), (1,)), ((), ())), preferred_element_type=F32)
            m = jnp.max(s, axis=-1, keepdims=True)
            e = jnp.exp(s - m)
            r = 1.0 / jnp.sum(e, axis=-1, keepdims=True)
            a = e[:tq] * r[:tq] - e[tq:] * (lam * r[tq:])
            o = jnp.dot(a.astype(BF16), v, preferred_element_type=F32)
            acc = jnp.where(grp64 == h, o, acc)
        o2 = acc * acc
        hi = o2.astype(BF16)
        lo = (o2 - hi.astype(F32)).astype(BF16)
        gi = lax.broadcasted_iota(jnp.int32, (256, 256), 0) >> 6
        gj = lax.broadcasted_iota(jnp.int32, (256, 256), 1) >> 6
        avg = jnp.where(gi == gj, 1.0 / 64.0, 0.0).astype(BF16)
        ms = (jnp.dot(hi, avg, preferred_element_type=F32)
              + jnp.dot(lo, avg, preferred_element_type=F32))
        o_ref[...] = (acc * lax.rsqrt(ms + EPS) * g_ref[...]).astype(BF16)

    @pl.when(j == 0)
    def _():
        body(ctx_len)

    @pl.when(j > 0)
    def _():
        body(ltot)


def _diff_call(p_all, lam, gscale, ctx_len):
    nb, ltot, _ = p_all.shape
    tq = ROW_TILE
    kern = functools.partial(_diff_kernel, ctx_len=ctx_len, ltot=ltot)
    return pl.pallas_call(
        kern,
        grid=(nb, ltot // tq),
        in_specs=[
            pl.BlockSpec(memory_space=pltpu.SMEM),
            pl.BlockSpec((None, tq, 256), lambda b, j: (b, j, COL_QD // 256)),
            pl.BlockSpec((None, ltot, 512), lambda b, j: (b, 0, COL_KD // 512)),
            pl.BlockSpec((1, 256), lambda b, j: (0, 0)),
        ],
        out_specs=pl.BlockSpec((None, tq, 256), lambda b, j: (b, j, 0)),
        out_shape=jax.ShapeDtypeStruct((nb, ltot, 256), BF16),
        compiler_params=pltpu.CompilerParams(
            dimension_semantics=("parallel", "arbitrary"), vmem_limit_bytes=VMEM_LIMIT),
        name="diff_attn",
    )(lam, p_all, p_all, gscale)


def _outproj_kernel(ya_ref, yb_ref, yd_ref, w_ref, x_ref, g1_ref, sh2_ref, sc2_ref, nw_ref, rt_ref,
                    x1_ref, h2_ref, aff_ref):
    ox = (jnp.dot(ya_ref[...], w_ref[0:512, :], preferred_element_type=F32)
          + jnp.dot(yb_ref[...], w_ref[512:768, :], preferred_element_type=F32)
          + jnp.dot(yd_ref[...], w_ref[768:1024, :], preferred_element_type=F32))
    x1 = x_ref[...] + g1_ref[...] * ox
    x1_ref[...] = x1
    ms = jnp.mean(x1 * x1, axis=-1, keepdims=True)
    h2 = (x1 * lax.rsqrt(ms + EPS) * nw_ref[...]) * (1.0 + sc2_ref[...]) + sh2_ref[...]
    hb = h2.astype(BF16)
    h2_ref[...] = hb
    logits = lax.dot_general(rt_ref[...], hb, (((1,), (1,)), ((), ())), preferred_element_type=F32)
    m = jnp.max(logits, axis=0, keepdims=True)
    e = jnp.exp(logits - m)
    aff_ref[...] = e / jnp.sum(e, axis=0, keepdims=True)


def _outproj_call(ya, yb, yd, w_out_b, x_all, mod_l, nw2, router_t):
    nb, ltot, d = x_all.shape
    tm = ROW_TILE

    def tile(width):
        return pl.BlockSpec((None, tm, width), lambda b, j: (b, j, 0))

    return pl.pallas_call(
        _outproj_kernel,
        grid=(nb, ltot // tm),
        in_specs=[
            tile(512), tile(256), tile(256),
            pl.BlockSpec((d, d), lambda b, j: (0, 0)),
            tile(d),
            _mod_spec(nb, 2), _mod_spec(nb, 3), _mod_spec(nb, 4),
            pl.BlockSpec((1, d), lambda b, j: (0, 0)),
            pl.BlockSpec((N_EXPERTS, d), lambda b, j: (0, 0)),
        ],
        out_specs=[
            tile(d), tile(d),
            pl.BlockSpec((None, N_EXPERTS, tm), lambda b, j: (b, 0, j)),
        ],
        out_shape=[
            jax.ShapeDtypeStruct((nb, ltot, d), F32),
            jax.ShapeDtypeStruct((nb, ltot, d), BF16),
            jax.ShapeDtypeStruct((nb, N_EXPERTS, ltot), F32),
        ],
        compiler_params=pltpu.CompilerParams(
            dimension_semantics=("parallel", "arbitrary"), vmem_limit_bytes=VMEM_LIMIT),
        name="out_proj",
    )(ya, yb, yd, w_out_b, x_all, mod_l, mod_l, mod_l, nw2, router_t)


def _expert_kernel(x_ref, wg_ref, wu_ref, wd_ref, y_ref, wgb, wub, wdb):
    @pl.when(pl.program_id(1) == 0)
    def _():
        wgb[...] = wg_ref[...].astype(BF16)
        wub[...] = wu_ref[...].astype(BF16)
        wdb[...] = wd_ref[...].astype(BF16)

    x = x_ref[...]
    a = jnp.dot(x, wgb[...], preferred_element_type=F32)
    u = jnp.dot(x, wub[...], preferred_element_type=F32)
    hidden = (a * (1.0 / (1.0 + jnp.exp(-a))) * u).astype(BF16)
    y_ref[...] = jnp.dot(hidden, wdb[...], preferred_element_type=F32)


def _expert_call(xs, w_gate, w_up, w_down, layer, tm):
    ne, rows, d = xs.shape
    ff = w_gate.shape[-1]

    def wspec(r, c):
        return pl.BlockSpec((None, None, r, c), lambda e, t: (layer, e, 0, 0))

    return pl.pallas_call(
        _expert_kernel,
        grid=(ne, rows // tm),
        in_specs=[
            pl.BlockSpec((None, tm, d), lambda e, t: (e, t, 0)),
            wspec(d, ff), wspec(d, ff), wspec(ff, d),
        ],
        out_specs=pl.BlockSpec((None, tm, d), lambda e, t: (e, t, 0)),
        out_shape=jax.ShapeDtypeStruct((ne, rows, d), F32),
        scratch_shapes=[pltpu.VMEM((d, ff), BF16), pltpu.VMEM((d, ff), BF16), pltpu.VMEM((ff, d), BF16)],
        compiler_params=pltpu.CompilerParams(
            dimension_semantics=("arbitrary", "arbitrary"), vmem_limit_bytes=VMEM_LIMIT),
        name="expert_ffn",
    )(xs, w_gate, w_up, w_down)


def _final_kernel(x_ref, g_ref, o_ref):
    x = x_ref[...]
    ms = jnp.mean(x * x, axis=-1, keepdims=True)
    o_ref[...] = x * lax.rsqrt(ms + EPS) * g_ref[...]


def _final_call(x_all, g, ctx_len):
    nb, ltot, d = x_all.shape
    tm = ROW_TILE
    off = ctx_len // tm
    return pl.pallas_call(
        _final_kernel,
        grid=(nb, (ltot - ctx_len) // tm),
        in_specs=[
            pl.BlockSpec((None, tm, d), lambda b, j: (b, j + off, 0)),
            pl.BlockSpec((1, d), lambda b, j: (0, 0)),
        ],
        out_specs=pl.BlockSpec((None, tm, d), lambda b, j: (b, j, 0)),
        out_shape=jax.ShapeDtypeStruct((nb, ltot - ctx_len, d), F32),
        compiler_params=pltpu.CompilerParams(
            dimension_semantics=("parallel", "arbitrary"), vmem_limit_bytes=VMEM_LIMIT),
        name="final_norm",
    )(x_all, g)


def kernel(x, c, ctx, c_ctx, w_ada, b_ada, norm1, norm2, w_in, w_out, attn_sink, conv_w, diff_lambda,
           diff_subln, router, w_gate, w_up, w_down, norm_final):
    nb, seq, d = x.shape
    ctx_len = ctx.shape[1]
    ltot = ctx_len + seq
    assert d == D_MODEL and ctx_len % ROW_TILE == 0 and seq % ROW_TILE == 0 and nb < MOD_ROWS

    x_all = jnp.concatenate([ctx, x], axis=1)
    cs = jnp.concatenate([c, c_ctx[None], jnp.zeros((MOD_ROWS - nb - 1, d), F32)], axis=0)
    mod = _ada_call(cs, w_ada, b_ada).reshape(DEPTH, MOD_ROWS, 1, 6 * d)
    tables = _rope_tables(ctx_len, seq)
    w_in_b = w_in[:, :, _in_col_perm()].astype(BF16)
    w_out_b = w_out[:, _out_row_perm(), :].astype(BF16)
    router_t = jnp.swapaxes(router, 1, 2).astype(BF16)
    cap_x = CAPACITY_FACTOR * seq // N_EXPERTS
    cap_c = CAPACITY_FACTOR * ctx_len // N_EXPERTS
    bidx = jnp.arange(nb)[None, :, None]

    for l in range(DEPTH):
        need_ctx = l < DEPTH - 1
        lam_init = 0.8 - 0.6 * math.exp(-0.3 * l)
        lp = diff_lambda[l].astype(F32)
        lam = (jnp.exp(jnp.sum(lp[0] * lp[1])) - jnp.exp(jnp.sum(lp[2] * lp[3])) + lam_init).reshape(1)
        gscale = (jnp.tile(diff_subln[l].astype(F32), N_HEADS_C) * (1.0 - lam_init)).reshape(1, 256)

        p_all = _inproj_call(x_all, mod[l], norm1[l].reshape(1, d), w_in_b[l], tables)
        ya = _win_call(p_all, attn_sink[l].astype(F32), ctx_len)
        yb = _conv_call(p_all, conv_w[l], ctx_len)
        yd = _diff_call(p_all, lam, gscale, ctx_len)
        x1, h2, aff_t = _outproj_call(ya, yb, yd, w_out_b[l], x_all, mod[l], norm2[l].reshape(1, d),
                                      router_t[l])

        g_x, i_x = lax.top_k(aff_t[:, :, ctx_len:], cap_x)
        rows, gates = i_x + ctx_len, g_x
        if need_ctx:
            g_c, i_c = lax.top_k(aff_t[:, :, :ctx_len], cap_c)
            rows = jnp.concatenate([i_c, rows], axis=-1)
            gates = jnp.concatenate([g_c, gates], axis=-1)
        slots = rows.shape[-1]
        rows_t = jnp.swapaxes(rows, 0, 1)
        gates_t = jnp.swapaxes(gates, 0, 1)
        xs = h2[bidx, rows_t].reshape(N_EXPERTS, nb * slots, d)
        y = _expert_call(xs, w_gate, w_up, w_down, l, 2 * slots)
        upd = y.reshape(N_EXPERTS, nb, slots, d) * gates_t[..., None]
        moe = jnp.zeros((nb, ltot, d), F32).at[bidx, rows_t].add(upd)
        g2 = jnp.concatenate([jnp.broadcast_to(mod[l, nb, :, 5 * d:], (nb, 1, d)),
                              mod[l, :nb, :, 5 * d:]], axis=1)
        g2_rows = jnp.concatenate([jnp.broadcast_to(g2[:, :1], (nb, ctx_len, d)),
                                   jnp.broadcast_to(g2[:, 1:], (nb, seq, d))], axis=1)
        x_all = x1 + g2_rows * moe

    return _final_call(x_all, norm_final.reshape(1, d), ctx_len)
```
